```python
import jax, jax.numpy as jnp
from jax import lax
import numpy as np

D_MODEL = 4096
BATCH = 4
SEQ = 2048
DEPTH = 2
DEC_BATCH = 128
DEC_SEQ = 8
PAST_LEN = 16384
PAGE_SIZE = 128

N_MIXERS = 2
N_GLA = (DEPTH + 1) // 2
N_SGU = DEPTH // 2
GLA_HEADS = 8
GLA_DK = D_MODEL // 2 // GLA_HEADS
GLA_DV = D_MODEL // GLA_HEADS
GLA_HK = GLA_HEADS * GLA_DK
GLA_HV = GLA_HEADS * GLA_DV
GLA_RANK = 16
GLA_GATE_NORM = 16.0
GLA_CHUNK = 32
GLA_IN = 2 * GLA_HK + 2 * GLA_HV + GLA_RANK
SGU_DV = 2 * D_MODEL
SGU_GROUPS = 16
SGU_CHUNK = 128
N_EXPERTS = 16
N_GROUPS = 4
E_PER_GROUP = N_EXPERTS // N_GROUPS
TOP_K = 2
D_EXPERT = D_MODEL // 4
ALPHA = (2 * DEPTH) ** 0.25
BETA = (8 * DEPTH) ** -0.25
LN_EPS = 1e-5

kernel_name = "hybrid_gla_sgu_groupmoe_step"


def layer_norm(x, g, b):
    xf = x.astype(jnp.float32)
    mu = xf.mean(-1, keepdims=True)
    var = jnp.square(xf - mu).mean(-1, keepdims=True)
    return ((xf - mu) * lax.rsqrt(var + LN_EPS)).astype(x.dtype) * g + b


def rms_norm(x, g):
    xf = x.astype(jnp.float32)
    y = xf * lax.rsqrt(jnp.mean(xf * xf, -1, keepdims=True) + LN_EPS)
    return y.astype(x.dtype) * g


def gla_chunked(q, k, v, logd, s0):
    out_dtype, st_dtype = v.dtype, s0.dtype
    B, T = q.shape[:2]
    C = min(GLA_CHUNK, T)
    pad = (-T) % C
    q, k, v, logd = [a.astype(jnp.float32) for a in (q, k, v, logd)]
    if pad:
        padw = ((0, 0), (0, pad), (0, 0), (0, 0))
        q, k, v, logd = [jnp.pad(a, padw) for a in (q, k, v, logd)]
    N = (T + pad) // C

    def to_chunks(a):
        return a.reshape(B, N, C, a.shape[2], a.shape[3]).transpose(1, 0, 3, 2, 4)

    mask = jnp.tril(jnp.ones((C, C), dtype=bool))

    def step(S, inp):
        qc, kc, vc, gc = inp
        b = jnp.cumsum(gc, axis=2)
        b_last = b[:, :, -1:]
        qe = qc * jnp.exp(b)
        ke = kc * jnp.exp(-b)
        A = jnp.where(mask, jnp.einsum('bhtk,bhsk->bhts', qe, ke), 0.0)
        o = jnp.einsum('bhtk,bhkv->bhtv', qe, S) + jnp.einsum('bhts,bhsv->bhtv', A, vc)
        kd = kc * jnp.exp(b_last - b)
        S = jnp.exp(b_last[:, :, 0])[..., None] * S + jnp.einsum('bhsk,bhsv->bhkv', kd, vc)
        return S, o

    S, o = lax.scan(step, s0.astype(jnp.float32), (to_chunks(q), to_chunks(k), to_chunks(v), to_chunks(logd)))
    o = o.transpose(1, 0, 3, 2, 4).reshape(B, N * C, GLA_HEADS, GLA_DV)[:, :T]
    return o.astype(out_dtype), S.astype(st_dtype)


def gla_mixer(x, s0, w_in, w_gk2, b_gk, norm_g, w_out):
    B, T, _ = x.shape
    proj = x @ w_in
    q, k, v, gate, lr = jnp.split(proj, [GLA_HK, 2 * GLA_HK, 2 * GLA_HK + GLA_HV, 2 * GLA_HK + 2 * GLA_HV], axis=-1)
    logd = jax.nn.log_sigmoid((lr @ w_gk2 + b_gk).astype(jnp.float32)) / GLA_GATE_NORM
    q = q.reshape(B, T, GLA_HEADS, GLA_DK) * (GLA_DK ** -0.5)
    k = k.reshape(B, T, GLA_HEADS, GLA_DK)
    v = v.reshape(B, T, GLA_HEADS, GLA_DV)
    logd = logd.reshape(B, T, GLA_HEADS, GLA_DK)
    o, s = gla_chunked(q, k, v, logd, s0)
    o = rms_norm(o, norm_g).reshape(B, T, GLA_HV) * jax.nn.silu(gate)
    return o @ w_out, s


def sgu_mixer(x, w_in, b_in, ln_g, ln_b, w_s, b_s, w_out):
    B, T, _ = x.shape
    z = jax.nn.gelu(x @ w_in + b_in)
    u, v = jnp.split(z, 2, axis=-1)
    v = layer_norm(v, ln_g, ln_b)
    pad = (-T) % SGU_CHUNK
    vp = jnp.pad(v, ((0, 0), (0, pad), (0, 0)))
    N = (T + pad) // SGU_CHUNK
    vc = vp.reshape(B, N, SGU_CHUNK, SGU_GROUPS, SGU_DV // SGU_GROUPS)
    ws = w_s * jnp.tril(jnp.ones((SGU_CHUNK, SGU_CHUNK), dtype=w_s.dtype))
    mixed = jnp.einsum('gts,bnsgc->bntgc', ws, vc) + b_s.T[None, None, :, :, None]
    mixed = mixed.reshape(B, N * SGU_CHUNK, SGU_DV)[:, :T]
    return (u * mixed) @ w_out, v


def moe(xt, router_w, router_b, w_gate, w_up, w_down):
    scores = jax.nn.sigmoid((xt @ router_w).astype(jnp.float32))
    sel = scores + router_b.astype(jnp.float32)
    gscore = lax.top_k(sel.reshape(-1, N_GROUPS, E_PER_GROUP), TOP_K)[0].sum(-1)
    gmask = jax.nn.one_hot(jnp.argmax(gscore, -1), N_GROUPS, dtype=bool)
    sel_m = jnp.where(jnp.repeat(gmask, E_PER_GROUP, axis=-1), sel, -jnp.inf)
    _, idx = lax.top_k(sel_m, TOP_K)
    w = jnp.take_along_axis(scores, idx, -1)
    w = w / w.sum(-1, keepdims=True)
    combine = jnp.einsum('tk,tke->te', w, jax.nn.one_hot(idx, N_EXPERTS, dtype=w.dtype)).astype(xt.dtype)
    y = jnp.zeros_like(xt)
    for e in range(N_EXPERTS):
        h = jax.nn.silu(xt @ w_gate[e]) * (xt @ w_up[e])
        y = y + combine[:, e:e + 1] * (h @ w_down[e])
    return y


def setup_inputs(seed: int = 0) -> dict:
    key = jax.random.key(seed)
    ks = jax.random.split(key, 32)
    n = lambda i, shape, s: jax.random.normal(ks[i], shape, jnp.float32) * s
    D, F = D_MODEL, D_EXPERT
    return {
        "x_prompt": n(0, (BATCH, SEQ, D), 1.0),
        "x_sample": n(1, (DEC_BATCH, DEC_SEQ, D), 1.0),
        "state_gla": n(2, (N_GLA, DEC_BATCH, GLA_HEADS, GLA_DK, GLA_DV), 0.5),
        "ln_mix_g": 1.0 + n(3, (DEPTH, D), 0.05),
        "ln_mix_b": n(4, (DEPTH, D), 0.02),
        "ln_ffn_g": 1.0 + n(5, (DEPTH, D), 0.05),
        "ln_ffn_b": n(6, (DEPTH, D), 0.02),
        "gla_w_in": n(7, (N_GLA, D, GLA_IN), D ** -0.5),
        "gla_w_gk2": n(8, (N_GLA, GLA_RANK, GLA_HK), GLA_RANK ** -0.5),
        "gla_b_gk": n(9, (N_GLA, GLA_HK), 0.1),
        "gla_norm_g": 1.0 + n(10, (N_GLA, GLA_HEADS, GLA_DV), 0.05),
        "gla_w_out": n(11, (N_GLA, GLA_HV, D), BETA * GLA_HV ** -0.5),
        "sgu_w_in": n(12, (N_SGU, D, 2 * SGU_DV), D ** -0.5),
        "sgu_b_in": n(13, (N_SGU, 2 * SGU_DV), 0.02),
        "sgu_ln_g": 1.0 + n(14, (N_SGU, SGU_DV), 0.05),
        "sgu_ln_b": n(15, (N_SGU, SGU_DV), 0.02),
        "sgu_w_s": n(16, (N_SGU, SGU_GROUPS, SGU_CHUNK, SGU_CHUNK), 0.5 * SGU_CHUNK ** -0.5),
        "sgu_b_s": 1.0 + n(17, (N_SGU, SGU_GROUPS, SGU_CHUNK), 0.1),
        "sgu_w_out": n(18, (N_SGU, SGU_DV, D), BETA * SGU_DV ** -0.5),
        "router_w": n(19, (D, N_EXPERTS), D ** -0.5),
        "router_b": n(20, (N_EXPERTS,), 0.01),
        "moe_w_gate": n(21, (DEPTH, N_EXPERTS, D, F), D ** -0.5),
        "moe_w_up": n(22, (DEPTH, N_EXPERTS, D, F), D ** -0.5),
        "moe_w_down": n(23, (DEPTH, N_EXPERTS, F, D), BETA * F ** -0.5),
    }


def reference(x_prompt, x_sample, state_gla, ln_mix_g, ln_mix_b, ln_ffn_g, ln_ffn_b,
              gla_w_in, gla_w_gk2, gla_b_gk, gla_norm_g, gla_w_out,
              sgu_w_in, sgu_b_in, sgu_ln_g, sgu_ln_b, sgu_w_s, sgu_b_s, sgu_w_out,
              router_w, router_b, moe_w_gate, moe_w_up, moe_w_down):
    xp, xs = x_prompt, x_sample
    n_p = xp.shape[0] * xp.shape[1]
    gla_sp, gla_ss, sgu_vs = [], [], []
    for i in range(DEPTH):
        j = i // N_MIXERS
        if i % N_MIXERS == 0:
            p = (gla_w_in[j], gla_w_gk2[j], gla_b_gk[j], gla_norm_g[j], gla_w_out[j])
            s_zero = jnp.zeros((xp.shape[0], GLA_HEADS, GLA_DK, GLA_DV), state_gla.dtype)
            mp, sp = gla_mixer(xp, s_zero, *p)
            ms, ss = gla_mixer(xs, state_gla[j], *p)
            gla_sp.append(sp)
            gla_ss.append(ss)
        else:
            p = (sgu_w_in[j], sgu_b_in[j], sgu_ln_g[j], sgu_ln_b[j], sgu_w_s[j], sgu_b_s[j], sgu_w_out[j])
            mp, _ = sgu_mixer(xp, *p)
            ms, vs = sgu_mixer(xs, *p)
            sgu_vs.append(vs)
        xp = layer_norm(ALPHA * xp + mp, ln_mix_g[i], ln_mix_b[i])
        xs = layer_norm(ALPHA * xs + ms, ln_mix_g[i], ln_mix_b[i])
        flat = jnp.concatenate([xp.reshape(-1, D_MODEL), xs.reshape(-1, D_MODEL)], axis=0)
        f = moe(flat, router_w, router_b, moe_w_gate[i], moe_w_up[i], moe_w_down[i])
        xp = layer_norm(ALPHA * xp + f[:n_p].reshape(xp.shape), ln_ffn_g[i], ln_ffn_b[i])
        xs = layer_norm(ALPHA * xs + f[n_p:].reshape(xs.shape), ln_ffn_g[i], ln_ffn_b[i])
    return (xp, xs, jnp.stack(gla_sp), jnp.stack(gla_ss), jnp.stack(sgu_vs))
```

```python
import functools

import jax
import jax.numpy as jnp
from jax import lax
from jax.experimental import pallas as pl
from jax.experimental.pallas import tpu as pltpu

F32 = jnp.float32
BF16 = jnp.bfloat16
I32 = jnp.int32

D_MODEL = 4096
BATCH = 4
SEQ = 2048
DEPTH = 2
DEC_BATCH = 128
DEC_SEQ = 8
N_PROMPT = BATCH * SEQ
N_SAMPLE = DEC_BATCH * DEC_SEQ
N_TOK = N_PROMPT + N_SAMPLE

GLA_HEADS = 8
GLA_DK = 256
GLA_DV = 512
GLA_HK = GLA_HEADS * GLA_DK
GLA_HV = GLA_HEADS * GLA_DV
GLA_RANK = 16
GLA_GATE_NORM = 16.0
GLA_PROJ = 2 * GLA_HK + 2 * GLA_HV
SGU_DV = 2 * D_MODEL
SGU_GROUPS = 16
SGU_CHUNK = 128
SGU_GC = SGU_DV // SGU_GROUPS
N_EXPERTS = 16
N_GROUPS = 4
E_PER_GROUP = 4
D_EXPERT = D_MODEL // 4
ALPHA = (2 * DEPTH) ** 0.25
LN_EPS = 1e-5

LANES = 128
VMEM_LIMIT = 56 * 1024 * 1024

GLA_CHUNK = 64
GLA_STEP = 256
GLA_SB = 8
MOE_TM = 256
MOE_ROWS = 2 * N_TOK + N_EXPERTS * MOE_TM
MOE_FB = 512
MOE_NB = 2048
ROW_TM = 256


def _cparams(sem, vmem=VMEM_LIMIT):
    return pltpu.CompilerParams(dimension_semantics=sem, vmem_limit_bytes=vmem)


def _iota(shape, dim):
    return lax.broadcasted_iota(I32, shape, dim)


def _dot(a, b):
    return jnp.dot(a, b, preferred_element_type=F32)


def _dot_nt(a, b):
    return lax.dot_general(a, b, (((1,), (1,)), ((), ())), preferred_element_type=F32)


def _dot_tn(a, b):
    return lax.dot_general(a, b, (((0,), (0,)), ((), ())), preferred_element_type=F32)


def _split2(a):
    hi = a.astype(BF16)
    lo = (a - hi.astype(F32)).astype(BF16)
    return hi, lo


def _dot3(a, b):
    ah, al = _split2(a)
    bh, bl = _split2(b)
    return _dot(ah, bh) + _dot(ah, bl) + _dot(al, bh)


def _sigmoid(x):
    return 1.0 / (1.0 + jnp.exp(-x))


def _layer_norm(x, g, b):
    mu = jnp.mean(x, axis=-1, keepdims=True)
    xc = x - mu
    var = jnp.mean(xc * xc, axis=-1, keepdims=True)
    return xc * lax.rsqrt(var + LN_EPS) * g + b


def _mm_kernel(*refs, nk, has_bias, act):
    if has_bias:
        x_ref, w_ref, b_ref, o_ref, acc_ref = refs
    else:
        x_ref, w_ref, o_ref, acc_ref = refs
        b_ref = None
    k = pl.program_id(2)
    p = _dot(x_ref[...], w_ref[...].astype(BF16))

    @pl.when(k == 0)
    def _():
        acc_ref[...] = p

    @pl.when(k > 0)
    def _():
        acc_ref[...] += p

    @pl.when(k == nk - 1)
    def _():
        r = acc_ref[...]
        if has_bias:
            r = r + b_ref[...]
        if act == "gelu":
            r = 0.5 * r * (1.0 + jnp.tanh(0.7978845608028654 * (r + 0.044715 * (r * r * r))))
        o_ref[...] = r.astype(o_ref.dtype)


def _matmul(x, w, *, name, n_cols=None, bias=None, act=None, out_dtype=F32, tm, tn, tk):
    m, kdim = x.shape
    n = w.shape[1] if n_cols is None else n_cols
    assert m % tm == 0 and n % tn == 0 and kdim % tk == 0
    nk = kdim // tk
    in_specs = [pl.BlockSpec((tm, tk), lambda i, j, k: (i, k)),
                pl.BlockSpec((tk, tn), lambda i, j, k: (k, j))]
    args = [x, w]
    if bias is not None:
        in_specs.append(pl.BlockSpec((1, tn), lambda i, j, k: (0, j)))
        args.append(bias.reshape(1, -1))
    return pl.pallas_call(
        functools.partial(_mm_kernel, nk=nk, has_bias=bias is not None, act=act),
        grid=(m // tm, n // tn, nk),
        in_specs=in_specs,
        out_specs=pl.BlockSpec((tm, tn), lambda i, j, k: (i, j)),
        out_shape=jax.ShapeDtypeStruct((m, n), out_dtype),
        scratch_shapes=[pltpu.VMEM((tm, tn), F32)],
        compiler_params=_cparams(("parallel", "parallel", "arbitrary")),
        name=name,
    )(*args)


def _ln_res_kernel(x_ref, m_ref, g_ref, b_ref, o_ref, ob_ref):
    y = _layer_norm(ALPHA * x_ref[...] + m_ref[...], g_ref[...], b_ref[...])
    o_ref[...] = y
    ob_ref[...] = y.astype(BF16)


def _ln_res(x, m, g, b):
    t = x.shape[0]
    row = pl.BlockSpec((ROW_TM, D_MODEL), lambda i: (i, 0))
    vec = pl.BlockSpec((1, D_MODEL), lambda i: (0, 0))
    return pl.pallas_call(
        _ln_res_kernel,
        grid=(t // ROW_TM,),
        in_specs=[row, row, vec, vec],
        out_specs=[row, row],
        out_shape=[jax.ShapeDtypeStruct((t, D_MODEL), F32), jax.ShapeDtypeStruct((t, D_MODEL), BF16)],
        compiler_params=_cparams(("parallel",)),
        name="ln_res",
    )(x, m, g.reshape(1, -1), b.reshape(1, -1))


def _gla_logd(lr, w_hi, w_lo, bgk):
    l_hi, l_lo = _split2(lr)
    z = _dot(l_hi, w_hi) + _dot(l_hi, w_lo) + _dot(l_lo, w_hi) + bgk
    return (jnp.minimum(z, 0.0) - jnp.log1p(jnp.exp(-jnp.abs(z)))) * (1.0 / GLA_GATE_NORM)


def _gla_out(o, ng, gate):
    ms = jnp.mean(o * o, axis=-1, keepdims=True)
    return (o * lax.rsqrt(ms + LN_EPS) * ng * (gate * _sigmoid(gate))).astype(BF16)


def _col_decay(d_hi, d_lo, ones):
    g = jnp.exp(_dot_tn(d_hi, ones) + _dot_tn(d_lo, ones))
    return jnp.concatenate([g] * (GLA_DV // LANES), axis=1)


def _gla_prompt_kernel(q_ref, k_ref, v_ref, gate_ref, lr_ref, wgk_ref, bgk_ref, ng_ref, og_in_ref,
                       og_ref, st_ref, s_ref):
    del og_in_ref
    t = pl.program_id(2)
    c = GLA_CHUNK

    @pl.when(t == 0)
    def _():
        s_ref[...] = jnp.zeros_like(s_ref)

    w_hi, w_lo = _split2(wgk_ref[...])
    bgk = bgk_ref[...]
    ng = ng_ref[...]
    causal = _iota((c, c), 1) <= _iota((c, c), 0)
    tri = causal.astype(BF16)
    ones = jnp.ones((c, LANES), BF16)
    for ci in range(GLA_STEP // c):
        sl = pl.ds(ci * c, c)
        logd = _gla_logd(lr_ref[sl, :], w_hi, w_lo, bgk)
        d_hi, d_lo = _split2(logd)
        b = _dot(tri, d_hi) + _dot(tri, d_lo)
        b_mid = b[c // 2 - 1:c // 2, :]
        b_last = b[c - 1:c, :]
        q = q_ref[sl, :].astype(F32) * (GLA_DK ** -0.5)
        k = k_ref[sl, :].astype(F32)
        v = v_ref[sl, :]
        qe = (q * jnp.exp(b - b_mid)).astype(BF16)
        ke = (k * jnp.exp(b_mid - b)).astype(BF16)
        qs = (q * jnp.exp(b)).astype(BF16)
        kd = (k * jnp.exp(b_last - b)).astype(BF16)
        a = jnp.where(causal, _dot_nt(qe, ke), 0.0).astype(BF16)
        s = s_ref[...]
        o = _dot(qs, s.astype(BF16)) + _dot(a, v)
        s_ref[...] = _col_decay(d_hi, d_lo, ones) * s + _dot_tn(kd, v)
        og_ref[sl, :] = _gla_out(o, ng, gate_ref[sl, :].astype(F32))

    @pl.when(t == pl.num_programs(2) - 1)
    def _():
        st_ref[0, 0] = s_ref[...]


def _gla_sample_kernel(q_ref, k_ref, v_ref, gate_ref, lr_ref, wgk_ref, bgk_ref, ng_ref, st_ref, og_in_ref,
                       og_ref, so_ref):
    del og_in_ref
    r = 2 * DEC_SEQ
    w_hi, w_lo = _split2(wgk_ref[...])
    bgk = bgk_ref[...]
    ng = ng_ref[...]
    same = (_iota((r, r), 0) // DEC_SEQ) == (_iota((r, r), 1) // DEC_SEQ)
    causal = same & (_iota((r, r), 1) <= _iota((r, r), 0))
    tri = causal.astype(BF16)
    ones = jnp.ones((r, LANES), BF16)
    first_k = _iota((r, GLA_DK), 0) < DEC_SEQ
    first_v = _iota((r, GLA_DV), 0) < DEC_SEQ
    q_all = q_ref[...].astype(F32) * (GLA_DK ** -0.5)
    k_all = k_ref[...].astype(F32)
    v_all = v_ref[...].astype(F32)
    gate_all = gate_ref[...].astype(F32)
    for p in range(GLA_SB // 2):
        sl = slice(p * r, (p + 1) * r)
        logd = _gla_logd(lr_ref[sl, :], w_hi, w_lo, bgk)
        d_hi, d_lo = _split2(logd)
        b = _dot(tri, d_hi) + _dot(tri, d_lo)
        b_last = jnp.where(first_k, b[DEC_SEQ - 1:DEC_SEQ, :], b[r - 1:r, :])
        q, k = q_all[sl], k_all[sl]
        v = v_all[sl].astype(BF16)
        qe = (q * jnp.exp(b)).astype(BF16)
        ke = (k * jnp.exp(-b)).astype(BF16)
        kd = k * jnp.exp(b_last - b)
        a = jnp.where(causal, _dot_nt(qe, ke), 0.0).astype(BF16)
        s0 = st_ref[2 * p, 0]
        s1 = st_ref[2 * p + 1, 0]
        o = jnp.where(first_v, _dot(qe, s0.astype(BF16)), _dot(qe, s1.astype(BF16))) + _dot(a, v)
        zero = jnp.zeros_like(kd)
        kd0 = jnp.where(first_k, kd, zero).astype(BF16)
        kd1 = jnp.where(first_k, zero, kd).astype(BF16)
        zb = jnp.zeros_like(d_hi)
        so_ref[2 * p, 0] = (_col_decay(jnp.where(first_k, d_hi, zb), jnp.where(first_k, d_lo, zb), ones) * s0
                            + _dot_tn(kd0, v))
        so_ref[2 * p + 1, 0] = (_col_decay(jnp.where(first_k, zb, d_hi), jnp.where(first_k, zb, d_lo), ones) * s1
                                + _dot_tn(kd1, v))
        og_ref[sl, :] = _gla_out(o, ng, gate_all[sl])


def _gla_mixer(xb, state, w_in, w_gk2, b_gk, norm_g, w_out):
    proj = _matmul(xb, w_in, name="gla_proj", n_cols=GLA_PROJ, out_dtype=BF16, tm=2304, tn=1024, tk=1024)
    w_lr = jnp.pad(w_in[:, GLA_PROJ:], ((0, 0), (0, LANES - GLA_RANK)))
    lr = _matmul(xb, w_lr, name="gla_lr", out_dtype=F32, tm=2304, tn=LANES, tk=1024)
    wgk = jnp.pad(w_gk2, ((0, LANES - GLA_RANK), (0, 0)))
    bgk = b_gk.reshape(1, GLA_HK)
    ng = norm_g.reshape(1, GLA_HV)

    kq, kv = GLA_HK // GLA_DK, (2 * GLA_HK) // GLA_DV
    kg = kv + GLA_HEADS
    steps = SEQ // GLA_STEP

    def tok_specs(rows, row_of):
        return [pl.BlockSpec((rows, GLA_DK), lambda *g: (row_of(*g), g[1])),
                pl.BlockSpec((rows, GLA_DK), lambda *g: (row_of(*g), kq + g[1])),
                pl.BlockSpec((rows, GLA_DV), lambda *g: (row_of(*g), kv + g[1])),
                pl.BlockSpec((rows, GLA_DV), lambda *g: (row_of(*g), kg + g[1])),
                pl.BlockSpec((rows, LANES), lambda *g: (row_of(*g), 0))]

    def head_specs(head_of):
        return [pl.BlockSpec((LANES, GLA_DK), lambda *g: (0, head_of(*g))),
                pl.BlockSpec((1, GLA_DK), lambda *g: (0, head_of(*g))),
                pl.BlockSpec((1, GLA_DV), lambda *g: (0, head_of(*g)))]

    og, st_p = pl.pallas_call(
        _gla_prompt_kernel,
        grid=(BATCH, GLA_HEADS, steps),
        in_specs=(tok_specs(GLA_STEP, lambda b, h, t: b * steps + t) + head_specs(lambda b, h, t: h)
                  + [pl.BlockSpec(memory_space=pl.ANY)]),
        out_specs=[pl.BlockSpec((GLA_STEP, GLA_DV), lambda b, h, t: (b * steps + t, h)),
                   pl.BlockSpec((1, 1, GLA_DK, GLA_DV), lambda b, h, t: (b, h, 0, 0))],
        out_shape=[jax.ShapeDtypeStruct((N_TOK, GLA_HV), BF16),
                   jax.ShapeDtypeStruct((BATCH, GLA_HEADS, GLA_DK, GLA_DV), F32)],
        scratch_shapes=[pltpu.VMEM((GLA_DK, GLA_DV), F32)],
        input_output_aliases={8: 0},
        compiler_params=_cparams(("parallel", "parallel", "arbitrary")),
        name="gla_prompt",
    )(proj, proj, proj, proj, lr, wgk, bgk, ng, jnp.zeros((N_TOK, GLA_HV), BF16))

    rows = GLA_SB * DEC_SEQ
    row0 = N_PROMPT // rows
    st_spec = pl.BlockSpec((GLA_SB, 1, GLA_DK, GLA_DV), lambda i, h: (i, h, 0, 0))
    og, st_s = pl.pallas_call(
        _gla_sample_kernel,
        grid=(DEC_BATCH // GLA_SB, GLA_HEADS),
        in_specs=(tok_specs(rows, lambda i, h: row0 + i) + head_specs(lambda i, h: h)
                  + [st_spec, pl.BlockSpec(memory_space=pl.ANY)]),
        out_specs=[pl.BlockSpec((rows, GLA_DV), lambda i, h: (row0 + i, h)), st_spec],
        out_shape=[jax.ShapeDtypeStruct((N_TOK, GLA_HV), BF16),
                   jax.ShapeDtypeStruct((DEC_BATCH, GLA_HEADS, GLA_DK, GLA_DV), F32)],
        input_output_aliases={9: 0},
        compiler_params=_cparams(("parallel", "parallel")),
        name="gla_sample",
    )(proj, proj, proj, proj, lr, wgk, bgk, ng, state, og)

    m = _matmul(og, w_out, name="gla_wout", out_dtype=F32, tm=2304, tn=1024, tk=512)
    return m, st_p, st_s


def _sgu_kernel(u_ref, v_ref, g_ref, b_ref, wm_ref, bm_ref, o_ref, vn_ref):
    vn = _layer_norm(v_ref[...].astype(F32), g_ref[...], b_ref[...])
    vn_ref[...] = vn
    for g in range(SGU_GROUPS):
        sl = slice(g * SGU_GC, (g + 1) * SGU_GC)
        mixed = _dot(wm_ref[0, g].astype(BF16), vn[:, sl].astype(BF16))
        mixed = mixed + jnp.concatenate([bm_ref[0, g]] * (SGU_GC // LANES), axis=1)
        o_ref[:, sl] = (u_ref[:, sl].astype(F32) * mixed).astype(BF16)


def _sgu_mixer(xb, w_in, b_in, ln_g, ln_b, w_s, b_s, w_out):
    z = _matmul(xb, w_in, name="sgu_proj", bias=b_in, act="gelu", out_dtype=BF16, tm=2304, tn=1024, tk=1024)
    c = SGU_CHUNK
    rep = c // DEC_SEQ
    tril = jnp.tril(jnp.ones((c, c), F32))
    w_prompt = w_s * tril
    corner = (w_s * tril)[:, :DEC_SEQ, :DEC_SEQ]
    w_sample = jnp.einsum("ab,gts->gatbs", jnp.eye(rep, dtype=F32), corner).reshape(SGU_GROUPS, c, c)
    wm = jnp.stack([w_prompt, w_sample])
    b_prompt = jnp.broadcast_to(b_s[:, :, None], (SGU_GROUPS, c, LANES))
    b_sample = jnp.broadcast_to(jnp.tile(b_s[:, :DEC_SEQ], (1, rep))[:, :, None], (SGU_GROUPS, c, LANES))
    bm = jnp.stack([b_prompt, b_sample])

    n_tiles = N_TOK // c
    p_tiles = N_PROMPT // c
    gmix, vn = pl.pallas_call(
        _sgu_kernel,
        grid=(n_tiles,),
        in_specs=[pl.BlockSpec((c, SGU_DV), lambda i: (i, 0)),
                  pl.BlockSpec((c, SGU_DV), lambda i: (i, 1)),
                  pl.BlockSpec((1, SGU_DV), lambda i: (0, 0)),
                  pl.BlockSpec((1, SGU_DV), lambda i: (0, 0)),
                  pl.BlockSpec((1, SGU_GROUPS, c, c), lambda i: (i // p_tiles, 0, 0, 0)),
                  pl.BlockSpec((1, SGU_GROUPS, c, LANES), lambda i: (i // p_tiles, 0, 0, 0))],
        out_specs=[pl.BlockSpec((c, SGU_DV), lambda i: (i, 0)),
                   pl.BlockSpec((c, SGU_DV), lambda i: (jnp.maximum(i - p_tiles, 0), 0))],
        out_shape=[jax.ShapeDtypeStruct((N_TOK, SGU_DV), BF16),
                   jax.ShapeDtypeStruct((N_SAMPLE, SGU_DV), F32)],
        compiler_params=_cparams(("arbitrary",)),
        name="sgu_mix",
    )(z, z, ln_g.reshape(1, -1), ln_b.reshape(1, -1), wm, bm)
    m = _matmul(gmix, w_out, name="sgu_wout", out_dtype=F32, tm=2304, tn=1024, tk=512)
    return m, vn


def _router_kernel(x_ref, rw_ref, rb_ref, eid_ref, wcol_ref):
    tm = x_ref.shape[0]
    logits = _dot3(x_ref[...], rw_ref[...])
    lt = logits.T
    rb = jnp.concatenate([rb_ref[...]] * (tm // LANES), axis=1)
    score = _sigmoid(lt[:N_EXPERTS, :])
    sel = score + rb
    neg = jnp.full((1, tm), -jnp.inf, F32)
    zero_i = jnp.zeros((1, tm), I32)

    def top2(vals):
        v1, i1 = vals[0], zero_i
        for j in range(1, len(vals)):
            c = vals[j] > v1
            v1 = jnp.where(c, vals[j], v1)
            i1 = jnp.where(c, j, i1)
        v2, i2 = neg, zero_i
        for j in range(len(vals)):
            c = (i1 != j) & (vals[j] > v2)
            v2 = jnp.where(c, vals[j], v2)
            i2 = jnp.where(c, j, i2)
        return v1, i1, v2, i2

    rows = [sel[e:e + 1, :] for e in range(N_EXPERTS)]
    srow = [score[e:e + 1, :] for e in range(N_EXPERTS)]
    per_group = [top2(rows[g * E_PER_GROUP:(g + 1) * E_PER_GROUP]) for g in range(N_GROUPS)]
    gscore = [pg[0] + pg[2] for pg in per_group]
    gbest, gi = gscore[0], zero_i
    for g in range(1, N_GROUPS):
        c = gscore[g] > gbest
        gbest = jnp.where(c, gscore[g], gbest)
        gi = jnp.where(c, g, gi)
    i1 = per_group[0][1]
    i2 = per_group[0][3]
    for g in range(1, N_GROUPS):
        i1 = jnp.where(gi == g, per_group[g][1], i1)
        i2 = jnp.where(gi == g, per_group[g][3], i2)
    e1 = gi * E_PER_GROUP + i1
    e2 = gi * E_PER_GROUP + i2
    s1 = jnp.zeros((1, tm), F32)
    s2 = jnp.zeros((1, tm), F32)
    for e in range(N_EXPERTS):
        s1 = jnp.where(e1 == e, srow[e], s1)
        s2 = jnp.where(e2 == e, srow[e], s2)
    tot = s1 + s2
    eid_ref[...] = jnp.concatenate([e1, e2] + [zero_i] * 6, axis=0)
    w = jnp.concatenate([s1 / tot, s2 / tot, jnp.zeros((LANES - 2, tm), F32)], axis=0)
    wcol_ref[...] = w.T


def _route(x, rw_pad, rb_b):
    tm = 512
    t = x.shape[0]
    return pl.pallas_call(
        _router_kernel,
        grid=(t // tm,),
        in_specs=[pl.BlockSpec((tm, D_MODEL), lambda i: (i, 0)),
                  pl.BlockSpec((D_MODEL, LANES), lambda i: (0, 0)),
                  pl.BlockSpec((N_EXPERTS, LANES), lambda i: (0, 0))],
        out_specs=[pl.BlockSpec((8, tm), lambda i: (0, i)),
                   pl.BlockSpec((tm, LANES), lambda i: (i, 0))],
        out_shape=[jax.ShapeDtypeStruct((8, t), I32), jax.ShapeDtypeStruct((t, LANES), F32)],
        compiler_params=_cparams(("parallel",)),
        name="moe_router",
    )(x, rw_pad, rb_b)


def _plan(eid):
    e = eid[:2].reshape(-1)
    onehot = (e[:, None] == jnp.arange(N_EXPERTS, dtype=I32)[None, :]).astype(I32)
    csum = jnp.cumsum(onehot, axis=0)
    rank = jnp.sum(csum * onehot, axis=1) - 1
    count = csum[-1]
    ntile = (count + MOE_TM - 1) // MOE_TM
    start = (jnp.cumsum(ntile) - ntile) * MOE_TM
    dest = start[e] + rank
    return dest.astype(I32), start.astype(I32), ntile.astype(I32)


def _dispatch_kernel(dest_ref, x_ref, xs_in_ref, xs_ref, sem):
    del xs_in_ref
    i = pl.program_id(0)
    n_tok = x_ref.shape[0]

    def issue(r, carry):
        t = i * ROW_TM + r
        src = x_ref.at[pl.ds(t, 1)]
        pltpu.make_async_copy(src, xs_ref.at[pl.ds(dest_ref[t], 1)], sem).start()
        pltpu.make_async_copy(src, xs_ref.at[pl.ds(dest_ref[n_tok + t], 1)], sem).start()
        return carry

    lax.fori_loop(0, ROW_TM, issue, 0)
    for _ in range(2):
        pltpu.make_async_copy(x_ref.at[pl.ds(0, ROW_TM)], xs_ref.at[pl.ds(0, ROW_TM)], sem).wait()


def _dispatch(x, dest):
    t = x.shape[0]
    return pl.pallas_call(
        _dispatch_kernel,
        grid_spec=pltpu.PrefetchScalarGridSpec(
            num_scalar_prefetch=1,
            grid=(t // ROW_TM,),
            in_specs=[pl.BlockSpec(memory_space=pl.ANY), pl.BlockSpec(memory_space=pl.ANY)],
            out_specs=pl.BlockSpec(memory_space=pl.ANY),
            scratch_shapes=[pltpu.SemaphoreType.DMA],
        ),
        out_shape=jax.ShapeDtypeStruct((MOE_ROWS, D_MODEL), F32),
        input_output_aliases={2: 0},
        compiler_params=_cparams(("arbitrary",)),
        name="moe_dispatch",
    )(dest, x, jnp.zeros((MOE_ROWS, D_MODEL), F32))


def _tile_rows(row0, i):
    return pl.ds(pl.multiple_of(row0 + i * MOE_TM, MOE_TM), MOE_TM)


def _expert_rows_loop(n, fetch, wait_in, compute, put, wait_out):
    @pl.when(n > 0)
    def _():
        fetch(0, 0)

    def body(i, carry):
        slot = i & 1
        wait_in(slot)

        @pl.when(i + 1 < n)
        def _():
            fetch(i + 1, 1 - slot)

        @pl.when(i >= 2)
        def _():
            wait_out(slot)

        compute(slot)
        put(i, slot)
        return carry

    lax.fori_loop(0, n, body, 0)

    @pl.when(n >= 2)
    def _():
        wait_out(n & 1)

    @pl.when(n >= 1)
    def _():
        wait_out((n - 1) & 1)


def _zero_tail(e, row0, n, obuf, out_copy):
    @pl.when(e == N_EXPERTS - 1)
    def _():
        obuf[0] = jnp.zeros(obuf.shape[1:], obuf.dtype)

        def body(i, carry):
            out_copy(n + i, 0).start()
            out_copy(0, 0).wait()
            return carry

        lax.fori_loop(0, (MOE_ROWS - (row0 + n * MOE_TM)) // MOE_TM, body, 0)


def _moe_up_kernel(start_ref, ntile_ref, wg_ref, wu_ref, xs_ref, hs_ref, wgb, wub, xbuf, hbuf, sem_in, sem_out):
    e = pl.program_id(0)
    f = pl.program_id(1)
    wgb[...] = wg_ref[0].astype(BF16)
    wub[...] = wu_ref[0].astype(BF16)
    row0 = pl.multiple_of(start_ref[e], MOE_TM)

    def in_copy(i, slot):
        return pltpu.make_async_copy(xs_ref.at[_tile_rows(row0, i)], xbuf.at[slot], sem_in.at[slot])

    def out_copy(i, slot):
        return pltpu.make_async_copy(
            hbuf.at[slot], hs_ref.at[_tile_rows(row0, i), pl.ds(pl.multiple_of(f * MOE_FB, MOE_FB), MOE_FB)], sem_out.at[slot])

    def compute(slot):
        x = xbuf[slot].astype(BF16)
        g = _dot(x, wgb[...])
        u = _dot(x, wub[...])
        hbuf[slot] = (g * _sigmoid(g) * u).astype(BF16)

    _expert_rows_loop(ntile_ref[e],
                      fetch=lambda i, s: in_copy(i, s).start(), wait_in=lambda s: in_copy(0, s).wait(),
                      compute=compute,
                      put=lambda i, s: out_copy(i, s).start(), wait_out=lambda s: out_copy(0, s).wait())
    _zero_tail(e, row0, ntile_ref[e], hbuf, out_copy)


def _moe_down_kernel(start_ref, ntile_ref, wd_ref, hs_ref, ys_ref, wdb, hbuf, ybuf, sem_in, sem_out):
    e = pl.program_id(0)
    j = pl.program_id(1)
    wdb[...] = wd_ref[0].astype(BF16)
    row0 = pl.multiple_of(start_ref[e], MOE_TM)

    def in_copy(i, slot):
        return pltpu.make_async_copy(hs_ref.at[_tile_rows(row0, i)], hbuf.at[slot], sem_in.at[slot])

    def out_copy(i, slot):
        return pltpu.make_async_copy(
            ybuf.at[slot], ys_ref.at[_tile_rows(row0, i), pl.ds(pl.multiple_of(j * MOE_NB, MOE_NB), MOE_NB)], sem_out.at[slot])

    def compute(slot):
        ybuf[slot] = _dot(hbuf[slot], wdb[...])

    _expert_rows_loop(ntile_ref[e],
                      fetch=lambda i, s: in_copy(i, s).start(), wait_in=lambda s: in_copy(0, s).wait(),
                      compute=compute,
                      put=lambda i, s: out_copy(i, s).start(), wait_out=lambda s: out_copy(0, s).wait())
    _zero_tail(e, row0, ntile_ref[e], ybuf, out_copy)


def _experts(xs, start, ntile, w_gate, w_up, w_down):
    hs = pl.pallas_call(
        _moe_up_kernel,
        grid_spec=pltpu.PrefetchScalarGridSpec(
            num_scalar_prefetch=2,
            grid=(N_EXPERTS, D_EXPERT // MOE_FB),
            in_specs=[pl.BlockSpec((1, D_MODEL, MOE_FB), lambda e, f, *_: (e, 0, f)),
                      pl.BlockSpec((1, D_MODEL, MOE_FB), lambda e, f, *_: (e, 0, f)),
                      pl.BlockSpec(memory_space=pl.ANY)],
            out_specs=pl.BlockSpec(memory_space=pl.ANY),
            scratch_shapes=[pltpu.VMEM((D_MODEL, MOE_FB), BF16), pltpu.VMEM((D_MODEL, MOE_FB), BF16),
                            pltpu.VMEM((2, MOE_TM, D_MODEL), F32), pltpu.VMEM((2, MOE_TM, MOE_FB), BF16),
                            pltpu.SemaphoreType.DMA((2,)), pltpu.SemaphoreType.DMA((2,))],
        ),
        out_shape=jax.ShapeDtypeStruct((MOE_ROWS, D_EXPERT), BF16),
        compiler_params=_cparams(("arbitrary", "arbitrary")),
        name="moe_up",
    )(start, ntile, w_gate, w_up, xs)
    return pl.pallas_call(
        _moe_down_kernel,
        grid_spec=pltpu.PrefetchScalarGridSpec(
            num_scalar_prefetch=2,
            grid=(N_EXPERTS, D_MODEL // MOE_NB),
            in_specs=[pl.BlockSpec((1, D_EXPERT, MOE_NB), lambda e, j, *_: (e, 0, j)),
                      pl.BlockSpec(memory_space=pl.ANY)],
            out_specs=pl.BlockSpec(memory_space=pl.ANY),
            scratch_shapes=[pltpu.VMEM((D_EXPERT, MOE_NB), BF16),
                            pltpu.VMEM((2, MOE_TM, D_EXPERT), BF16), pltpu.VMEM((2, MOE_TM, MOE_NB), F32),
                            pltpu.SemaphoreType.DMA((2,)), pltpu.SemaphoreType.DMA((2,))],
        ),
        out_shape=jax.ShapeDtypeStruct((MOE_ROWS, D_MODEL), F32),
        compiler_params=_cparams(("arbitrary", "arbitrary")),
        name="moe_down",
    )(start, ntile, w_down, hs)


def _combine_kernel(dest_ref, x_ref, w_ref, g_ref, b_ref, ys_ref, o_ref, ob_ref, y1, y2, sem):
    i = pl.program_id(0)
    n_tok = pl.num_programs(0) * ROW_TM

    def issue(r, carry):
        t = i * ROW_TM + r
        pltpu.make_async_copy(ys_ref.at[pl.ds(dest_ref[t], 1)], y1.at[pl.ds(r, 1)], sem).start()
        pltpu.make_async_copy(ys_ref.at[pl.ds(dest_ref[n_tok + t], 1)], y2.at[pl.ds(r, 1)], sem).start()
        return carry

    lax.fori_loop(0, ROW_TM, issue, 0)
    pltpu.make_async_copy(ys_ref.at[pl.ds(0, ROW_TM)], y1, sem).wait()
    pltpu.make_async_copy(ys_ref.at[pl.ds(0, ROW_TM)], y2, sem).wait()
    w = w_ref[...]
    f = w[:, 0:1] * y1[...] + w[:, 1:2] * y2[...]
    y = _layer_norm(ALPHA * x_ref[...] + f, g_ref[...], b_ref[...])
    o_ref[...] = y
    ob_ref[...] = y.astype(BF16)


def _combine(x, wcol, dest, ys, g, b):
    t = x.shape[0]
    row = pl.BlockSpec((ROW_TM, D_MODEL), lambda i, *_: (i, 0))
    vec = pl.BlockSpec((1, D_MODEL), lambda i, *_: (0, 0))
    return pl.pallas_call(
        _combine_kernel,
        grid_spec=pltpu.PrefetchScalarGridSpec(
            num_scalar_prefetch=1,
            grid=(t // ROW_TM,),
            in_specs=[row, pl.BlockSpec((ROW_TM, LANES), lambda i, *_: (i, 0)), vec, vec,
                      pl.BlockSpec(memory_space=pl.ANY)],
            out_specs=[row, row],
            scratch_shapes=[pltpu.VMEM((ROW_TM, D_MODEL), F32), pltpu.VMEM((ROW_TM, D_MODEL), F32),
                            pltpu.SemaphoreType.DMA],
        ),
        out_shape=[jax.ShapeDtypeStruct((t, D_MODEL), F32), jax.ShapeDtypeStruct((t, D_MODEL), BF16)],
        compiler_params=_cparams(("arbitrary",)),
        name="moe_combine",
    )(dest, x, wcol, g.reshape(1, -1), b.reshape(1, -1), ys)


def _moe_block(x, rw_pad, rb_b, w_gate, w_up, w_down, g, b):
    eid, wcol = _route(x, rw_pad, rb_b)
    dest, start, ntile = _plan(eid)
    xs = _dispatch(x, dest)
    ys = _experts(xs, start, ntile, w_gate, w_up, w_down)
    return _combine(x, wcol, dest, ys, g, b)


def kernel(x_prompt, x_sample, state_gla, ln_mix_g, ln_mix_b, ln_ffn_g, ln_ffn_b, gla_w_in, gla_w_gk2, gla_b_gk,
           gla_norm_g, gla_w_out, sgu_w_in, sgu_b_in, sgu_ln_g, sgu_ln_b, sgu_w_s, sgu_b_s, sgu_w_out, router_w,
           router_b, moe_w_gate, moe_w_up, moe_w_down):
    x = jnp.concatenate([x_prompt.reshape(N_PROMPT, D_MODEL), x_sample.reshape(N_SAMPLE, D_MODEL)], axis=0)
    xb = x.astype(BF16)
    rw_pad = jnp.pad(router_w, ((0, 0), (0, LANES - N_EXPERTS)))
    rb_b = jnp.broadcast_to(router_b[:, None], (N_EXPERTS, LANES))

    m, st_p, st_s = _gla_mixer(xb, state_gla[0], gla_w_in[0], gla_w_gk2[0], gla_b_gk[0], gla_norm_g[0],
                               gla_w_out[0])
    x, xb = _ln_res(x, m, ln_mix_g[0], ln_mix_b[0])
    x, xb = _moe_block(x, rw_pad, rb_b, moe_w_gate[0], moe_w_up[0], moe_w_down[0], ln_ffn_g[0], ln_ffn_b[0])

    m, vn = _sgu_mixer(xb, sgu_w_in[0], sgu_b_in[0], sgu_ln_g[0], sgu_ln_b[0], sgu_w_s[0], sgu_b_s[0],
                       sgu_w_out[0])
    x, xb = _ln_res(x, m, ln_mix_g[1], ln_mix_b[1])
    x, xb = _moe_block(x, rw_pad, rb_b, moe_w_gate[1], moe_w_up[1], moe_w_down[1], ln_ffn_g[1], ln_ffn_b[1])

    return (x[:N_PROMPT].reshape(BATCH, SEQ, D_MODEL),
            x[N_PROMPT:].reshape(DEC_BATCH, DEC_SEQ, D_MODEL),
            st_p[None], st_s[None],
            vn.reshape(1, DEC_BATCH, DEC_SEQ, SGU_DV))
```
